```python
import math
import jax, jax.numpy as jnp
from jax import lax
import numpy as np

D_MODEL = 1024
BATCH = 8
SEQ = 2048
DEPTH = 2
DEC_BATCH = 32
DEC_SEQ = 8
PAST_LEN = 16384
PAGE_SIZE = 128

N_EVEN = (DEPTH + 1) // 2
N_ODD = DEPTH // 2

S5_WIDTH = 512
S5_GROUP = 16
S5_GROUPS = S5_WIDTH // S5_GROUP
S5_STATE = 64
S5_DT_MIN = 1e-3
S5_DT_MAX = 1e-1
SSD_INNER = 512
SSD_HEADDIM = 64
SSD_HEADS = SSD_INNER // SSD_HEADDIM
SSD_GROUPS = 2
SSD_STATE = 128
SSD_CONV = 4
SSD_CHUNK = 128
SSD_CONV_DIM = SSD_INNER + 2 * SSD_GROUPS * SSD_STATE
IN0_WIDTH = S5_WIDTH + SSD_INNER + SSD_CONV_DIM + SSD_HEADS
MIX0_WIDTH = S5_WIDTH + SSD_INNER
MLA_HEADS = 8
QK_NOPE = 128
QK_ROPE = 64
V_HEAD = 128
Q_LORA = 384
KV_LORA = 256
IN1_WIDTH = Q_LORA + KV_LORA + QK_ROPE
MIX1_WIDTH = MLA_HEADS * V_HEAD
ROPE_BASE = 10000.0
Q_BLOCK = 128
SM_SCALE = (QK_NOPE + QK_ROPE) ** -0.5
D_FF = 2816
FFN_CONV = 3
ALPHA = (2 * DEPTH) ** 0.25
BETA = (8 * DEPTH) ** -0.25
EPS = 1e-5

kernel_name = 'hybrid_s5_ssd_mla_convffn_step'


def layer_norm(x, g, b):
    xf = x.astype(jnp.float32)
    mu = jnp.mean(xf, -1, keepdims=True)
    var = jnp.mean(jnp.square(xf - mu), -1, keepdims=True)
    return ((xf - mu) * lax.rsqrt(var + EPS) * g + b).astype(x.dtype)


def rms_norm(x, g):
    xf = x.astype(jnp.float32)
    return (xf * lax.rsqrt(jnp.mean(xf * xf, -1, keepdims=True) + EPS) * g).astype(x.dtype)


def group_rms_norm(y, g):
    bsz, L, w = y.shape
    yf = y.astype(jnp.float32).reshape(bsz, L, SSD_GROUPS, w // SSD_GROUPS)
    yf = yf * lax.rsqrt(jnp.mean(yf * yf, -1, keepdims=True) + EPS)
    return (yf.reshape(bsz, L, w) * g).astype(y.dtype)


def causal_dwconv(x, buf, w, b):
    xp = jnp.concatenate([buf.astype(x.dtype), x], axis=1)
    y = lax.conv_general_dilated(xp, w[:, None, :], window_strides=(1,), padding='VALID',
                                 dimension_numbers=('NWC', 'WIO', 'NWC'),
                                 feature_group_count=x.shape[-1])
    k1 = w.shape[0] - 1
    return y + b, xp[:, xp.shape[1] - k1:]


def s5_mix(u, h0_re, h0_im, a_re, a_im, log_dt, b_re, b_im, c_re, c_im, d, w_glu, b_glu):
    bsz, L, _ = u.shape
    f32 = jnp.float32
    ug = u.reshape(bsz, L, S5_GROUPS, S5_GROUP)
    ar, ai = a_re.astype(f32), a_im.astype(f32)
    step = jnp.exp(log_dt.astype(f32))[:, None]
    mag = jnp.exp(ar * step)
    lam_re, lam_im = mag * jnp.cos(ai * step), mag * jnp.sin(ai * step)
    den = ar * ar + ai * ai
    f_re = ((lam_re - 1.0) * ar + lam_im * ai) / den
    f_im = (lam_im * ar - (lam_re - 1.0) * ai) / den
    br, bi = b_re.astype(f32), b_im.astype(f32)
    bb_re = (f_re[..., None] * br - f_im[..., None] * bi).astype(u.dtype)
    bb_im = (f_re[..., None] * bi + f_im[..., None] * br).astype(u.dtype)
    bu_re = jnp.einsum('gnh,blgh->blgn', bb_re, ug)
    bu_im = jnp.einsum('gnh,blgh->blgn', bb_im, ug)
    lr = jnp.broadcast_to(lam_re.astype(u.dtype), bu_re.shape)
    li = jnp.broadcast_to(lam_im.astype(u.dtype), bu_re.shape)

    def combine(e1, e2):
        a1r, a1i, b1r, b1i = e1
        a2r, a2i, b2r, b2i = e2
        return (a2r * a1r - a2i * a1i, a2r * a1i + a2i * a1r,
                a2r * b1r - a2i * b1i + b2r, a2r * b1i + a2i * b1r + b2i)

    pr, pi, sr, si = lax.associative_scan(combine, (lr, li, bu_re, bu_im), axis=1)
    h0r, h0i = h0_re.astype(u.dtype)[:, None], h0_im.astype(u.dtype)[:, None]
    h_re = sr + pr * h0r - pi * h0i
    h_im = si + pr * h0i + pi * h0r
    y = (jnp.einsum('ghn,blgn->blgh', c_re, h_re) - jnp.einsum('ghn,blgn->blgh', c_im, h_im)
         + d * ug).reshape(bsz, L, S5_WIDTH)
    g = jax.nn.gelu(y)
    out = g * jax.nn.sigmoid(g @ w_glu + b_glu)
    return out, h_re[:, -1], h_im[:, -1]


def ssd_scan(x, dt, a, bm, cm, h0):
    bsz, L, H, P = x.shape
    T = min(SSD_CHUNK, L)
    nc = -(-L // T)
    pad = nc * T - L
    if pad:
        padw = lambda t: jnp.pad(t, [(0, 0), (0, pad)] + [(0, 0)] * (t.ndim - 2))
        x, dt, bm, cm = padw(x), padw(dt), padw(bm), padw(cm)
    G, N = SSD_GROUPS, bm.shape[-1]
    R = H // G
    x = x.reshape(bsz, nc, T, G, R, P)
    dt = dt.reshape(bsz, nc, T, G, R)
    bm = bm.reshape(bsz, nc, T, G, N)
    cm = cm.reshape(bsz, nc, T, G, N)
    cs = jnp.cumsum((dt * a.reshape(G, R)).astype(jnp.float32), axis=2)
    causal = jnp.tril(jnp.ones((T, T), bool))[None, None, :, :, None, None]
    seg = cs[:, :, :, None] - cs[:, :, None, :]
    decay = jnp.exp(jnp.where(causal, seg, -jnp.inf)).astype(x.dtype)
    xdt = x * dt[..., None]
    cb = jnp.einsum('bctgn,bcsgn->bctsg', cm, bm)
    y_diag = jnp.einsum('bctsgr,bcsgrp->bctgrp', cb[..., None] * decay, xdt)
    to_end = jnp.exp(cs[:, :, -1:] - cs).astype(x.dtype)
    states = jnp.einsum('bcsgn,bcsgrp->bcgrpn', bm, xdt * to_end[..., None])
    chunk_decay = jnp.exp(cs[:, :, -1]).astype(x.dtype)

    def step(h, inp):
        st, dec = inp
        return h * dec[..., None, None] + st, h

    h_last, h_prev = lax.scan(step, h0.astype(x.dtype).reshape(bsz, G, R, P, N),
                              (jnp.moveaxis(states, 1, 0), jnp.moveaxis(chunk_decay, 1, 0)))
    h_prev = jnp.moveaxis(h_prev, 0, 1)
    y_off = jnp.einsum('bctgn,bcgrpn->bctgrp', cm, h_prev) * jnp.exp(cs).astype(x.dtype)[..., None]
    y = (y_diag + y_off).reshape(bsz, nc * T, H, P)[:, :L]
    return y, h_last.reshape(bsz, H, P, N)


def ssd_mix(z, xbc, dt_raw, conv_buf, h0, conv_w, conv_b, dt_bias, a_log, d_skip, norm_g):
    xbc, new_buf = causal_dwconv(xbc, conv_buf, conv_w, conv_b)
    xbc = jax.nn.silu(xbc)
    bsz, L, _ = xbc.shape
    o1 = SSD_INNER + SSD_GROUPS * SSD_STATE
    xs = xbc[..., :SSD_INNER].reshape(bsz, L, SSD_HEADS, SSD_HEADDIM)
    bm = xbc[..., SSD_INNER:o1].reshape(bsz, L, SSD_GROUPS, SSD_STATE)
    cm = xbc[..., o1:].reshape(bsz, L, SSD_GROUPS, SSD_STATE)
    dt = jax.nn.softplus(dt_raw + dt_bias)
    a = -jnp.exp(a_log)
    y, h_last = ssd_scan(xs, dt, a, bm, cm, h0)
    y = (y + d_skip[:, None] * xs).reshape(bsz, L, SSD_INNER) * jax.nn.silu(z)
    return group_rms_norm(y, norm_g), new_buf, h_last


def rope_tables(pos):
    inv = ROPE_BASE ** (-jnp.arange(0, QK_ROPE, 2, dtype=jnp.float32) / QK_ROPE)
    ang = pos.astype(jnp.float32)[:, None] * inv
    return jnp.cos(ang), jnp.sin(ang)


def apply_rope(x, cos, sin):
    half = QK_ROPE // 2
    x1, x2 = x[..., :half], x[..., half:]
    return jnp.concatenate([x1 * cos - x2 * sin, x1 * sin + x2 * cos], -1).astype(x.dtype)


def mla_project(x, pos, w_in, g_q, g_kv, w_uq, w_uk):
    bsz, L, _ = x.shape
    h = x @ w_in
    cq = rms_norm(h[..., :Q_LORA], g_q)
    ckv = rms_norm(h[..., Q_LORA:Q_LORA + KV_LORA], g_kv)
    q = (cq @ w_uq).reshape(bsz, L, MLA_HEADS, QK_NOPE + QK_ROPE)
    cos, sin = rope_tables(pos)
    q_rope = apply_rope(q[..., QK_NOPE:], cos[:, None, :], sin[:, None, :])
    k_rope = apply_rope(h[..., Q_LORA + KV_LORA:], cos, sin)
    q_lat = jnp.einsum('blhd,chd->blhc', q[..., :QK_NOPE], w_uk)
    return q_lat, q_rope, ckv, k_rope


def mla_attend(q_lat, q_rope, ckv, kr):
    nq, nk = q_lat.shape[1], ckv.shape[1]
    s = jnp.einsum('bqhc,bkc->bhqk', q_lat, ckv) + jnp.einsum('bqhr,bkr->bhqk', q_rope, kr)
    visible = jnp.arange(nk)[None, :] <= (nk - nq + jnp.arange(nq))[:, None]
    s = jnp.where(visible, s.astype(jnp.float32) * SM_SCALE, -jnp.inf)
    pr = jax.nn.softmax(s, axis=-1).astype(ckv.dtype)
    return jnp.einsum('bhqk,bkc->bqhc', pr, ckv)


def conv_ffn(x, buf, w_up, conv_w, conv_b, w_down):
    h = x @ w_up
    h, new_buf = causal_dwconv(h, buf, conv_w, conv_b)
    return (jax.nn.silu(h[..., :D_FF]) * h[..., D_FF:]) @ w_down, new_buf


def run_group(x, pos0, s5_re0, s5_im0, ssd_h0, ssd_conv0, ffn_conv0, cache_ckv, cache_krope, page_table, p):
    bsz, L, _ = x.shape
    pos = pos0 + jnp.arange(L, dtype=jnp.int32)
    s5_re, s5_im, ssd_h, ssd_cv, ckv_rows, kr_rows, ffn_cv = [], [], [], [], [], [], []
    for i in range(DEPTH):
        j = i // 2
        if i % 2 == 0:
            h = x @ p['w_in0'][j]
            o0 = S5_WIDTH
            o1 = o0 + SSD_INNER
            o2 = o1 + SSD_CONV_DIM
            y5, hr, hi = s5_mix(h[..., :o0], s5_re0[j], s5_im0[j], p['s5_a_re'][j], p['s5_a_im'][j],
                                p['s5_log_dt'][j], p['s5_b_re'][j], p['s5_b_im'][j], p['s5_c_re'][j],
                                p['s5_c_im'][j], p['s5_d'][j], p['s5_w_glu'][j], p['s5_b_glu'][j])
            ys, cbuf, hs = ssd_mix(h[..., o0:o1], h[..., o1:o2], h[..., o2:], ssd_conv0[j], ssd_h0[j],
                                   p['ssd_conv_w'][j], p['ssd_conv_b'][j], p['ssd_dt_bias'][j],
                                   p['ssd_a_log'][j], p['ssd_d'][j], p['ssd_norm_g'][j])
            mix = jnp.concatenate([y5, ys], axis=-1) @ p['w_out0'][j]
            s5_re.append(hr)
            s5_im.append(hi)
            ssd_h.append(hs)
            ssd_cv.append(cbuf)
        else:
            q_lat, q_rope, ckv, kr = mla_project(x, pos, p['w_in1'][j], p['mla_q_norm_g'][j],
                                                 p['mla_kv_norm_g'][j], p['mla_w_uq'][j], p['mla_w_uk'][j])
            if page_table is None:
                o_lat = jnp.concatenate(
                    [mla_attend(q_lat[:, s:s + Q_BLOCK], q_rope[:, s:s + Q_BLOCK],
                                ckv[:, :s + Q_BLOCK], kr[:, :s + Q_BLOCK]) for s in range(0, L, Q_BLOCK)],
                    axis=1)
            else:
                n_past = page_table.shape[1] * PAGE_SIZE
                past_ckv = cache_ckv[page_table, j].reshape(bsz, n_past, KV_LORA).astype(ckv.dtype)
                past_kr = cache_krope[page_table, j].reshape(bsz, n_past, QK_ROPE).astype(kr.dtype)
                o_lat = mla_attend(q_lat, q_rope, jnp.concatenate([past_ckv, ckv], axis=1),
                                   jnp.concatenate([past_kr, kr], axis=1))
            o = jnp.einsum('blhc,chd->blhd', o_lat, p['mla_w_uv'][j]).reshape(bsz, L, MIX1_WIDTH)
            mix = o @ p['w_out1'][j]
            ckv_rows.append(ckv)
            kr_rows.append(kr)
        x = layer_norm(ALPHA * x + mix, p['ln1_g'][i], p['ln1_b'][i])
        f, fbuf = conv_ffn(x, ffn_conv0[i], p['ffn_w_up'][i], p['ffn_conv_w'][i], p['ffn_conv_b'][i],
                           p['ffn_w_down'][i])
        ffn_cv.append(fbuf)
        x = layer_norm(ALPHA * x + f, p['ln2_g'][i], p['ln2_b'][i])
    return x, (jnp.stack(s5_re), jnp.stack(s5_im), jnp.stack(ssd_h), jnp.stack(ssd_cv),
               jnp.stack(ckv_rows, axis=1), jnp.stack(kr_rows, axis=1), jnp.stack(ffn_cv))


def setup_inputs(seed: int = 0) -> dict:
    key = jax.random.key(seed)
    ks = iter(jax.random.split(key, 64))

    def nrm(shape, scale):
        return jax.random.normal(next(ks), shape, jnp.float32) * scale

    NE, NO = N_EVEN, N_ODD
    n_pages = PAST_LEN // PAGE_SIZE
    n_used = DEC_BATCH * n_pages
    n_pool = n_used + n_used // 4
    x_prompt = nrm((BATCH, SEQ, D_MODEL), 1.0)
    x_sample = nrm((DEC_BATCH, DEC_SEQ, D_MODEL), 1.0)
    state_s5_re = nrm((NE, DEC_BATCH, S5_GROUPS, S5_STATE), 0.5)
    state_s5_im = nrm((NE, DEC_BATCH, S5_GROUPS, S5_STATE), 0.5)
    state_ssd = nrm((NE, DEC_BATCH, SSD_HEADS, SSD_HEADDIM, SSD_STATE), 0.3)
    state_ssd_conv = nrm((NE, DEC_BATCH, SSD_CONV - 1, SSD_CONV_DIM), 1.0)
    state_ffn_conv = nrm((DEPTH, DEC_BATCH, FFN_CONV - 1, 2 * D_FF), 1.0)
    cache_ckv = nrm((n_pool, NO, PAGE_SIZE, KV_LORA), 1.0)
    cache_krope = nrm((n_pool, NO, PAGE_SIZE, QK_ROPE), 1.0)
    page_table = jax.random.permutation(next(ks), n_pool)[:n_used].reshape(DEC_BATCH, n_pages).astype(jnp.int32)
    w_in0 = nrm((NE, D_MODEL, IN0_WIDTH), D_MODEL ** -0.5)
    s5_a_re = -0.5 + nrm((NE, S5_GROUPS, S5_STATE), 0.01)
    s5_a_im = math.pi * jnp.arange(S5_STATE, dtype=jnp.float32) + nrm((NE, S5_GROUPS, S5_STATE), 0.01)
    s5_log_dt = jax.random.uniform(next(ks), (NE, S5_GROUPS), jnp.float32,
                                   minval=math.log(S5_DT_MIN), maxval=math.log(S5_DT_MAX))
    s5_b_re = nrm((NE, S5_GROUPS, S5_STATE, S5_GROUP), (2 * S5_GROUP) ** -0.5)
    s5_b_im = nrm((NE, S5_GROUPS, S5_STATE, S5_GROUP), (2 * S5_GROUP) ** -0.5)
    s5_c_re = nrm((NE, S5_GROUPS, S5_GROUP, S5_STATE), (2 * S5_STATE) ** -0.5)
    s5_c_im = nrm((NE, S5_GROUPS, S5_GROUP, S5_STATE), (2 * S5_STATE) ** -0.5)
    s5_d = nrm((NE, S5_GROUPS, S5_GROUP), 0.5)
    s5_w_glu = nrm((NE, S5_WIDTH, S5_WIDTH), S5_WIDTH ** -0.5)
    s5_b_glu = nrm((NE, S5_WIDTH), 0.02)
    ssd_conv_w = nrm((NE, SSD_CONV, SSD_CONV_DIM), SSD_CONV ** -0.5)
    ssd_conv_b = nrm((NE, SSD_CONV_DIM), 0.02)
    dt0 = jnp.exp(jax.random.uniform(next(ks), (NE, SSD_HEADS), jnp.float32,
                                     minval=math.log(1e-3), maxval=math.log(1e-1)))
    ssd_dt_bias = dt0 + jnp.log(-jnp.expm1(-dt0))
    ssd_a_log = jnp.log(jax.random.uniform(next(ks), (NE, SSD_HEADS), jnp.float32, minval=1.0, maxval=16.0))
    ssd_d = 1.0 + nrm((NE, SSD_HEADS), 0.01)
    ssd_norm_g = 1.0 + nrm((NE, SSD_INNER), 0.01)
    w_out0 = nrm((NE, MIX0_WIDTH, D_MODEL), BETA * MIX0_WIDTH ** -0.5)
    w_in1 = nrm((NO, D_MODEL, IN1_WIDTH), D_MODEL ** -0.5)
    mla_q_norm_g = 1.0 + nrm((NO, Q_LORA), 0.01)
    mla_kv_norm_g = 1.0 + nrm((NO, KV_LORA), 0.01)
    mla_w_uq = nrm((NO, Q_LORA, MLA_HEADS * (QK_NOPE + QK_ROPE)), Q_LORA ** -0.5)
    mla_w_uk = nrm((NO, KV_LORA, MLA_HEADS, QK_NOPE), KV_LORA ** -0.5)
    mla_w_uv = nrm((NO, KV_LORA, MLA_HEADS, V_HEAD), KV_LORA ** -0.5)
    w_out1 = nrm((NO, MIX1_WIDTH, D_MODEL), BETA * MIX1_WIDTH ** -0.5)
    ln1_g = 1.0 + nrm((DEPTH, D_MODEL), 0.01)
    ln1_b = nrm((DEPTH, D_MODEL), 0.01)
    ffn_w_up = nrm((DEPTH, D_MODEL, 2 * D_FF), D_MODEL ** -0.5)
    ffn_conv_w = nrm((DEPTH, FFN_CONV, 2 * D_FF), FFN_CONV ** -0.5)
    ffn_conv_b = nrm((DEPTH, 2 * D_FF), 0.02)
    ffn_w_down = nrm((DEPTH, D_FF, D_MODEL), BETA * D_FF ** -0.5)
    ln2_g = 1.0 + nrm((DEPTH, D_MODEL), 0.01)
    ln2_b = nrm((DEPTH, D_MODEL), 0.01)
    return {'x_prompt': x_prompt, 'x_sample': x_sample,
            'state_s5_re': state_s5_re, 'state_s5_im': state_s5_im, 'state_ssd': state_ssd,
            'state_ssd_conv': state_ssd_conv, 'state_ffn_conv': state_ffn_conv,
            'cache_ckv': cache_ckv, 'cache_krope': cache_krope, 'page_table': page_table,
            'w_in0': w_in0, 's5_a_re': s5_a_re, 's5_a_im': s5_a_im, 's5_log_dt': s5_log_dt,
            's5_b_re': s5_b_re, 's5_b_im': s5_b_im, 's5_c_re': s5_c_re, 's5_c_im': s5_c_im,
            's5_d': s5_d, 's5_w_glu': s5_w_glu, 's5_b_glu': s5_b_glu,
            'ssd_conv_w': ssd_conv_w, 'ssd_conv_b': ssd_conv_b, 'ssd_dt_bias': ssd_dt_bias,
            'ssd_a_log': ssd_a_log, 'ssd_d': ssd_d, 'ssd_norm_g': ssd_norm_g, 'w_out0': w_out0,
            'w_in1': w_in1, 'mla_q_norm_g': mla_q_norm_g, 'mla_kv_norm_g': mla_kv_norm_g,
            'mla_w_uq': mla_w_uq, 'mla_w_uk': mla_w_uk, 'mla_w_uv': mla_w_uv, 'w_out1': w_out1,
            'ln1_g': ln1_g, 'ln1_b': ln1_b, 'ffn_w_up': ffn_w_up, 'ffn_conv_w': ffn_conv_w,
            'ffn_conv_b': ffn_conv_b, 'ffn_w_down': ffn_w_down, 'ln2_g': ln2_g, 'ln2_b': ln2_b}


def reference(x_prompt, x_sample, state_s5_re, state_s5_im, state_ssd, state_ssd_conv, state_ffn_conv,
              cache_ckv, cache_krope, page_table,
              w_in0, s5_a_re, s5_a_im, s5_log_dt, s5_b_re, s5_b_im, s5_c_re, s5_c_im, s5_d, s5_w_glu, s5_b_glu,
              ssd_conv_w, ssd_conv_b, ssd_dt_bias, ssd_a_log, ssd_d, ssd_norm_g, w_out0,
              w_in1, mla_q_norm_g, mla_kv_norm_g, mla_w_uq, mla_w_uk, mla_w_uv, w_out1,
              ln1_g, ln1_b, ffn_w_up, ffn_conv_w, ffn_conv_b, ffn_w_down, ln2_g, ln2_b):
    p = dict(w_in0=w_in0, s5_a_re=s5_a_re, s5_a_im=s5_a_im, s5_log_dt=s5_log_dt, s5_b_re=s5_b_re,
             s5_b_im=s5_b_im, s5_c_re=s5_c_re, s5_c_im=s5_c_im, s5_d=s5_d, s5_w_glu=s5_w_glu,
             s5_b_glu=s5_b_glu, ssd_conv_w=ssd_conv_w, ssd_conv_b=ssd_conv_b, ssd_dt_bias=ssd_dt_bias,
             ssd_a_log=ssd_a_log, ssd_d=ssd_d, ssd_norm_g=ssd_norm_g, w_out0=w_out0,
             w_in1=w_in1, mla_q_norm_g=mla_q_norm_g, mla_kv_norm_g=mla_kv_norm_g, mla_w_uq=mla_w_uq,
             mla_w_uk=mla_w_uk, mla_w_uv=mla_w_uv, w_out1=w_out1,
             ln1_g=ln1_g, ln1_b=ln1_b, ffn_w_up=ffn_w_up, ffn_conv_w=ffn_conv_w, ffn_conv_b=ffn_conv_b,
             ffn_w_down=ffn_w_down, ln2_g=ln2_g, ln2_b=ln2_b)
    bp = x_prompt.shape[0]
    dtp = x_prompt.dtype
    y_prompt, (s5_re_p, s5_im_p, ssd_p, ssd_conv_p, ckv_p, krope_p, ffn_conv_p) = run_group(
        x_prompt, 0,
        jnp.zeros((N_EVEN, bp, S5_GROUPS, S5_STATE), dtp), jnp.zeros((N_EVEN, bp, S5_GROUPS, S5_STATE), dtp),
        jnp.zeros((N_EVEN, bp, SSD_HEADS, SSD_HEADDIM, SSD_STATE), dtp),
        jnp.zeros((N_EVEN, bp, SSD_CONV - 1, SSD_CONV_DIM), dtp),
        jnp.zeros((DEPTH, bp, FFN_CONV - 1, 2 * D_FF), dtp),
        None, None, None, p)
    past_len = page_table.shape[1] * PAGE_SIZE
    y_sample, (s5_re_s, s5_im_s, ssd_s, ssd_conv_s, ckv_s, krope_s, ffn_conv_s) = run_group(
        x_sample, past_len, state_s5_re, state_s5_im, state_ssd, state_ssd_conv, state_ffn_conv,
        cache_ckv, cache_krope, page_table, p)
    return (y_prompt, y_sample,
            s5_re_p, s5_im_p, ssd_p, ssd_conv_p, ckv_p, krope_p, ffn_conv_p,
            s5_re_s, s5_im_s, ssd_s, ssd_conv_s, ckv_s, krope_s, ffn_conv_s)
```

```python
import functools
import math

import jax
import jax.numpy as jnp
from jax import lax
from jax.experimental import pallas as pl
from jax.experimental.pallas import tpu as pltpu

F32 = jnp.float32
BF16 = jnp.bfloat16

D_MODEL = 1024
S5_WIDTH = 512
S5_GROUP = 16
S5_GROUPS = 32
S5_STATE = 64
S5_NSTATE = S5_GROUPS * S5_STATE
SSD_INNER = 512
SSD_HEADDIM = 64
SSD_HEADS = 8
SSD_GROUPS = 2
SSD_STATE = 128
SSD_CONV = 4
SSD_CHUNK = 128
SSD_CONV_DIM = 1024
IN0_PAD = 2176
MLA_HEADS = 8
QK_NOPE = 128
QK_ROPE = 64
V_HEAD = 128
Q_LORA = 384
KV_LORA = 256
IN1_PAD = 768
QK_CAT = KV_LORA + 128
ROPE_BASE = 10000.0
SM_SCALE = (QK_NOPE + QK_ROPE) ** -0.5
D_FF = 2816
FFN_CONV = 3
DEPTH = 2
ALPHA = (2 * DEPTH) ** 0.25
EPS = 1e-5
PAGE_SIZE = 128

V7X_VMEM_BYTES = 64 * 1024 * 1024
NEG_BIG = -1e30


def _cparams(semantics, vmem_mib):
    assert vmem_mib * 1024 * 1024 < V7X_VMEM_BYTES
    return pltpu.CompilerParams(dimension_semantics=semantics, vmem_limit_bytes=vmem_mib * 1024 * 1024)


def _dot(a, b):
    return jnp.dot(a, b, preferred_element_type=F32)


def _dot_nt(a, b):
    return lax.dot_general(a, b, (((1,), (1,)), ((), ())), preferred_element_type=F32)


def _dot_tn(a, b):
    return lax.dot_general(a, b, (((0,), (0,)), ((), ())), preferred_element_type=F32)


def _split3(x):
    p1 = x.astype(BF16)
    r1 = x - p1.astype(F32)
    p2 = r1.astype(BF16)
    p3 = (r1 - p2.astype(F32)).astype(BF16)
    return p1, p2, p3


def _sigmoid(x):
    return 1.0 / (1.0 + jnp.exp(-x))


def _softplus(x):
    return jnp.maximum(x, 0.0) + jnp.log1p(jnp.exp(-jnp.abs(x)))


def _gelu_tanh(x):
    return x * (0.5 * (1.0 + jnp.tanh(math.sqrt(2.0 / math.pi) * (x + 0.044715 * (x * x * x)))))


def _layer_norm(v, g, b):
    mu = jnp.mean(v, -1, keepdims=True)
    d = v - mu
    var = jnp.mean(d * d, -1, keepdims=True)
    return d * lax.rsqrt(var + EPS) * g + b


def _rms_norm(v, g):
    return v * lax.rsqrt(jnp.mean(v * v, -1, keepdims=True) + EPS) * g


def _linear_kernel(x_ref, w_ref, o_ref):
    o_ref[...] = _dot(x_ref[...].astype(BF16), w_ref[...])


def _linear(x, w, tm):
    m, k = x.shape
    n = w.shape[1]
    return pl.pallas_call(
        _linear_kernel,
        grid=(m // tm,),
        in_specs=[pl.BlockSpec((tm, k), lambda i: (i, 0)), pl.BlockSpec((k, n), lambda i: (0, 0))],
        out_specs=pl.BlockSpec((tm, n), lambda i: (i, 0)),
        out_shape=jax.ShapeDtypeStruct((m, n), F32),
        compiler_params=_cparams(("parallel",), 40),
        name="in_proj0",
    )(x, w)


S5_NCHUNK = 4
S5_CH = S5_NSTATE // S5_NCHUNK
S5_ROWS = 8


def _s5_kernel(u_ref, h0r_ref, h0i_ref, wb_ref, tab_ref, wc_ref, d_ref, wg_ref, bg_ref,
               y_ref, hr_ref, hi_ref, bu_ref, st_ref, *, tc, nchunks):
    c = pl.program_id(1)

    @pl.when(c == 0)
    def _():
        for j in range(S5_NCHUNK):
            st_ref[:, 2 * S5_CH * j:2 * S5_CH * j + S5_CH] = jnp.broadcast_to(
                h0r_ref[0, :, S5_CH * j:S5_CH * (j + 1)], (S5_ROWS, S5_CH))
            st_ref[:, 2 * S5_CH * j + S5_CH:2 * S5_CH * (j + 1)] = jnp.broadcast_to(
                h0i_ref[0, :, S5_CH * j:S5_CH * (j + 1)], (S5_ROWS, S5_CH))

    u = u_ref[0]
    ub = u.astype(BF16)
    for j in range(S5_NCHUNK):
        bu_ref[:, 2 * S5_CH * j:2 * S5_CH * (j + 1)] = _dot(ub[:, 128 * j:128 * (j + 1)], wb_ref[j])

    def scan_rows(i, carry):
        r0 = pl.multiple_of(i * S5_ROWS, S5_ROWS)
        for j in range(S5_NCHUNK):
            cr = slice(2 * S5_CH * j, 2 * S5_CH * j + S5_CH)
            ci = slice(2 * S5_CH * j + S5_CH, 2 * S5_CH * (j + 1))
            xr = bu_ref[pl.ds(r0, S5_ROWS), cr]
            xi = bu_ref[pl.ds(r0, S5_ROWS), ci]
            for k, dist in enumerate((1, 2, 4)):
                mr = tab_ref[k, :, cr]
                mi = tab_ref[k, :, ci]
                sr = pltpu.roll(xr, dist, 0)
                si = pltpu.roll(xi, dist, 0)
                xr, xi = xr + (mr * sr - mi * si), xi + (mr * si + mi * sr)
            pr = tab_ref[3, :, cr]
            pi_ = tab_ref[3, :, ci]
            hr0 = st_ref[:, cr]
            hi0 = st_ref[:, ci]
            hr = xr + (pr * hr0 - pi_ * hi0)
            hi = xi + (pr * hi0 + pi_ * hr0)
            bu_ref[pl.ds(r0, S5_ROWS), cr] = hr
            bu_ref[pl.ds(r0, S5_ROWS), ci] = hi
            st_ref[:, cr] = jnp.broadcast_to(hr[S5_ROWS - 1:S5_ROWS], (S5_ROWS, S5_CH))
            st_ref[:, ci] = jnp.broadcast_to(hi[S5_ROWS - 1:S5_ROWS], (S5_ROWS, S5_CH))
        return carry

    lax.fori_loop(0, tc // S5_ROWS, scan_rows, 0)

    ys = []
    for j in range(S5_NCHUNK):
        hb = bu_ref[:, 2 * S5_CH * j:2 * S5_CH * (j + 1)].astype(BF16)
        ys.append(_dot(hb, wc_ref[j]))
    y = jnp.concatenate(ys, axis=1) + d_ref[...] * u
    g = _gelu_tanh(y)
    y_ref[0] = g * _sigmoid(_dot(g.astype(BF16), wg_ref[...]) + bg_ref[...])

    @pl.when(c == nchunks - 1)
    def _():
        for j in range(S5_NCHUNK):
            hr_ref[0, :, S5_CH * j:S5_CH * (j + 1)] = st_ref[0:1, 2 * S5_CH * j:2 * S5_CH * j + S5_CH]
            hi_ref[0, :, S5_CH * j:S5_CH * (j + 1)] = st_ref[0:1, 2 * S5_CH * j + S5_CH:2 * S5_CH * (j + 1)]


def _s5_call(h0arr, st_re, st_im, prm, tc):
    nseq, seq_len, _ = h0arr.shape
    nchunks = seq_len // tc
    const3 = lambda b, c: (0, 0, 0)
    const2 = lambda b, c: (0, 0)
    return pl.pallas_call(
        functools.partial(_s5_kernel, tc=tc, nchunks=nchunks),
        grid=(nseq, nchunks),
        in_specs=[
            pl.BlockSpec((1, tc, S5_WIDTH), lambda b, c: (b, c, 0)),
            pl.BlockSpec((1, 1, S5_NSTATE), lambda b, c: (b, 0, 0)),
            pl.BlockSpec((1, 1, S5_NSTATE), lambda b, c: (b, 0, 0)),
            pl.BlockSpec((S5_NCHUNK, 128, 2 * S5_CH), const3),
            pl.BlockSpec((4, S5_ROWS, 2 * S5_NSTATE), const3),
            pl.BlockSpec((S5_NCHUNK, 2 * S5_CH, 128), const3),
            pl.BlockSpec((1, S5_WIDTH), const2),
            pl.BlockSpec((S5_WIDTH, S5_WIDTH), const2),
            pl.BlockSpec((1, S5_WIDTH), const2),
        ],
        out_specs=[
            pl.BlockSpec((1, tc, S5_WIDTH), lambda b, c: (b, c, 0)),
            pl.BlockSpec((1, 1, S5_NSTATE), lambda b, c: (b, 0, 0)),
            pl.BlockSpec((1, 1, S5_NSTATE), lambda b, c: (b, 0, 0)),
        ],
        out_shape=[
            jax.ShapeDtypeStruct((nseq, seq_len, S5_WIDTH), F32),
            jax.ShapeDtypeStruct((nseq, 1, S5_NSTATE), F32),
            jax.ShapeDtypeStruct((nseq, 1, S5_NSTATE), F32),
        ],
        scratch_shapes=[pltpu.VMEM((tc, 2 * S5_NSTATE), F32), pltpu.VMEM((S5_ROWS, 2 * S5_NSTATE), F32)],
        compiler_params=_cparams(("parallel", "arbitrary"), 40),
        name="s5_mixer",
    )(h0arr, st_re, st_im, prm["wb"], prm["tab"], prm["wc"], prm["d"], prm["wg"], prm["bg"])


def _chunk_layout(re, im):
    r = re.shape[0]
    return jnp.concatenate([re.reshape(r, S5_NCHUNK, S5_CH), im.reshape(r, S5_NCHUNK, S5_CH)], axis=-1).reshape(
        r, 2 * S5_NSTATE)


def _s5_prepare(a_re, a_im, log_dt, b_re, b_im, c_re, c_im, d, w_glu, b_glu):
    step = jnp.exp(log_dt)[:, None]
    mag = jnp.exp(a_re * step)
    lam_re, lam_im = mag * jnp.cos(a_im * step), mag * jnp.sin(a_im * step)
    den = a_re * a_re + a_im * a_im
    f_re = ((lam_re - 1.0) * a_re + lam_im * a_im) / den
    f_im = (lam_im * a_re - (lam_re - 1.0) * a_im) / den
    bb_re = f_re[..., None] * b_re - f_im[..., None] * b_im
    bb_im = f_re[..., None] * b_im + f_im[..., None] * b_re
    eye = jnp.eye(8, dtype=F32)
    gpc = S5_GROUPS // S5_NCHUNK

    def bdiag_b(bb):
        a = bb.reshape(S5_NCHUNK, gpc, S5_STATE, S5_GROUP)
        return jnp.einsum("ab,janh->jahbn", eye, a).reshape(S5_NCHUNK, gpc * S5_GROUP, S5_CH)

    def bdiag_c(cc):
        a = cc.reshape(S5_NCHUNK, gpc, S5_GROUP, S5_STATE)
        return jnp.einsum("ab,jahn->jbnah", eye, a).reshape(S5_NCHUNK, S5_CH, gpc * S5_GROUP)

    wb = jnp.concatenate([bdiag_b(bb_re), bdiag_b(bb_im)], axis=-1).astype(BF16)
    wc = jnp.concatenate([bdiag_c(c_re), -bdiag_c(c_im)], axis=1).astype(BF16)

    def cmul(p, q):
        return p[0] * q[0] - p[1] * q[1], p[0] * q[1] + p[1] * q[0]

    l1 = (lam_re.reshape(-1), lam_im.reshape(-1))
    l2 = cmul(l1, l1)
    l4 = cmul(l2, l2)
    pows = [l1]
    for _ in range(S5_ROWS - 1):
        pows.append(cmul(pows[-1], l1))
    rows = jnp.arange(S5_ROWS)[:, None]
    tabs = []
    for dist, lam_d in ((1, l1), (2, l2), (4, l4)):
        keep = (rows >= dist).astype(F32)
        tabs.append(_chunk_layout(keep * lam_d[0][None, :], keep * lam_d[1][None, :]))
    tabs.append(_chunk_layout(jnp.stack([p[0] for p in pows]), jnp.stack([p[1] for p in pows])))
    return dict(wb=wb, wc=wc, tab=jnp.stack(tabs), d=d.reshape(1, S5_WIDTH), wg=w_glu.astype(BF16),
                bg=b_glu.reshape(1, S5_WIDTH))


def _ssd_kernel(z_ref, xbc_ref, dt_ref, cbuf_ref, h0_ref, cw_ref, cb_ref, dtb_ref, a_ref, dsk_ref, ng_ref,
                tril_ref, ee_ref, y_ref, hout_ref, carry_ref, st_ref, *, t, l_real, nchunks):
    c = pl.program_id(1)
    hp = SSD_INNER
    gw = hp // SSD_GROUPS

    @pl.when(c == 0)
    def _():
        carry_ref[...] = cbuf_ref[0]
        st_ref[...] = h0_ref[0].T

    xp = jnp.concatenate([carry_ref[...], xbc_ref[0]], axis=0)
    cw = cw_ref[...]
    conv = cb_ref[...] + cw[SSD_CONV - 1:SSD_CONV] * xp[8:]
    for k in range(SSD_CONV - 1):
        conv = conv + cw[k:k + 1] * pltpu.roll(xp, SSD_CONV - 1 - k, 0)[8:]
    carry_ref[...] = xp[t:]
    xbc = conv * _sigmoid(conv)
    xs = xbc[:, :hp]

    dt = _softplus(dt_ref[0] + dtb_ref[...])
    if l_real < t:
        dt = jnp.where(lax.broadcasted_iota(jnp.int32, dt.shape, 0) < l_real, dt, 0.0)
    da = dt * a_ref[...]
    tril = tril_ref[...]
    cs = sum(_dot(tril, p) for p in _split3(da))
    ee = ee_ref[...]
    cs_all = sum(_dot(p, ee) for p in _split3(cs))
    cs_exp = cs_all[:, :hp]
    cs_col = cs_all[:, hp:]
    dt_exp = sum(_dot(p, ee[:, :hp]) for p in _split3(dt))
    cs_t = cs.T
    ecs = jnp.exp(cs_exp)
    cs_last = cs_exp[t - 1:t, :]
    to_end = jnp.exp(cs_last - cs_exp)
    cdec = jnp.exp(cs_last)
    xdt = xs * dt_exp
    causal = lax.broadcasted_iota(jnp.int32, (t, t), 0) >= lax.broadcasted_iota(jnp.int32, (t, t), 1)
    lane = lax.broadcasted_iota(jnp.int32, (t, 128), 1)
    hpg = SSD_HEADS // SSD_GROUPS

    ys = []
    for g in range(SSD_GROUPS):
        bg = xbc[:, hp + SSD_STATE * g:hp + SSD_STATE * (g + 1)].astype(BF16)
        cg = xbc[:, hp + SSD_STATE * (SSD_GROUPS + g):hp + SSD_STATE * (SSD_GROUPS + g + 1)].astype(BF16)
        cb = _dot_nt(cg, bg)
        stg = st_ref[:, gw * g:gw * (g + 1)]
        yoff = _dot(cg, stg.astype(BF16)) * ecs[:, gw * g:gw * (g + 1)]
        for q in range(hpg // 2):
            slab = 2 * g + q
            xq = xdt[:, 128 * slab:128 * (slab + 1)].astype(BF16)
            acc = None
            for half in range(2):
                h = hpg * g + 2 * q + half
                seg = cs_col[:, 128 * h:128 * h + t] - cs_t[h:h + 1, :]
                dec = jnp.exp(jnp.where(causal, seg, -jnp.inf))
                lm = (cb * dec).astype(BF16)
                keep = (lane < SSD_HEADDIM) if half == 0 else (lane >= SSD_HEADDIM)
                part = _dot(lm, jnp.where(keep, xq, jnp.zeros_like(xq)))
                acc = part if acc is None else acc + part
            ys.append(acc + yoff[:, 128 * q:128 * (q + 1)])
        xw = (xdt[:, gw * g:gw * (g + 1)] * to_end[:, gw * g:gw * (g + 1)]).astype(BF16)
        st_ref[:, gw * g:gw * (g + 1)] = stg * cdec[:, gw * g:gw * (g + 1)] + _dot_tn(bg, xw)

    y = jnp.concatenate(ys, axis=1) + dsk_ref[...] * xs
    zz = z_ref[0]
    y = y * (zz * _sigmoid(zz))
    outs = []
    for g in range(SSD_GROUPS):
        yg = y[:, gw * g:gw * (g + 1)]
        outs.append(yg * lax.rsqrt(jnp.mean(yg * yg, -1, keepdims=True) + EPS))
    y_ref[0] = jnp.concatenate(outs, axis=1) * ng_ref[...]

    @pl.when(c == nchunks - 1)
    def _():
        hout_ref[0] = st_ref[...].T


def _ssd_call(h0arr, l_real, cbuf, state0, prm):
    nseq, lp, _ = h0arr.shape
    t = SSD_CHUNK
    nchunks = lp // t
    assert nchunks == 1 or l_real == t
    c2 = lambda b, c: (0, 0)
    return pl.pallas_call(
        functools.partial(_ssd_kernel, t=t, l_real=l_real, nchunks=nchunks),
        grid=(nseq, nchunks),
        in_specs=[
            pl.BlockSpec((1, t, SSD_INNER), lambda b, c: (b, c, 1)),
            pl.BlockSpec((1, t, SSD_CONV_DIM), lambda b, c: (b, c, 1)),
            pl.BlockSpec((1, t, 128), lambda b, c: (b, c, 2048 // 128)),
            pl.BlockSpec((1, 8, SSD_CONV_DIM), lambda b, c: (b, 0, 0)),
            pl.BlockSpec((1, SSD_INNER, SSD_STATE), lambda b, c: (b, 0, 0)),
            pl.BlockSpec((SSD_CONV, SSD_CONV_DIM), c2),
            pl.BlockSpec((1, SSD_CONV_DIM), c2),
            pl.BlockSpec((1, 128), c2),
            pl.BlockSpec((1, 128), c2),
            pl.BlockSpec((1, SSD_INNER), c2),
            pl.BlockSpec((1, SSD_INNER), c2),
            pl.BlockSpec((t, t), c2),
            pl.BlockSpec((128, SSD_INNER + SSD_HEADS * 128), c2),
        ],
        out_specs=[
            pl.BlockSpec((1, t, SSD_INNER), lambda b, c: (b, c, 0)),
            pl.BlockSpec((1, SSD_INNER, SSD_STATE), lambda b, c: (b, 0, 0)),
        ],
        out_shape=[
            jax.ShapeDtypeStruct((nseq, lp, SSD_INNER), F32),
            jax.ShapeDtypeStruct((nseq, SSD_INNER, SSD_STATE), F32),
        ],
        scratch_shapes=[pltpu.VMEM((8, SSD_CONV_DIM), F32), pltpu.VMEM((SSD_STATE, SSD_INNER), F32)],
        compiler_params=_cparams(("parallel", "arbitrary"), 40),
        name="ssd_mixer",
    )(h0arr, h0arr, h0arr, cbuf, state0, prm["cw"], prm["cb"], prm["dtb"], prm["a"], prm["dsk"], prm["ng"],
      prm["tril"], prm["ee"])


def _ssd_prepare(conv_w, conv_b, dt_bias, a_log, d_skip, norm_g):
    t = SSD_CHUNK
    pad = 128 - SSD_HEADS
    heads = jnp.arange(128)[:, None]
    e1 = (heads == (jnp.arange(SSD_INNER)[None, :] // SSD_HEADDIM)).astype(BF16)
    e2 = (heads == (jnp.arange(SSD_HEADS * 128)[None, :] // 128)).astype(BF16)
    return dict(
        cw=conv_w, cb=conv_b.reshape(1, SSD_CONV_DIM),
        dtb=jnp.pad(dt_bias, (0, pad)).reshape(1, 128),
        a=jnp.pad(-jnp.exp(a_log), (0, pad)).reshape(1, 128),
        dsk=jnp.repeat(d_skip, SSD_HEADDIM).reshape(1, SSD_INNER),
        ng=norm_g.reshape(1, SSD_INNER),
        tril=jnp.tril(jnp.ones((t, t), F32)).astype(BF16),
        ee=jnp.concatenate([e1, e2], axis=1),
    )


def _proj_ln_kernel(*refs, n_lhs):
    lhs = refs[:n_lhs]
    ws = refs[n_lhs:2 * n_lhs]
    res_ref, g_ref, b_ref, o_ref = refs[2 * n_lhs:]
    mix = None
    for a_ref, w_ref in zip(lhs, ws):
        part = _dot(a_ref[...].astype(BF16), w_ref[...])
        mix = part if mix is None else mix + part
    o_ref[...] = _layer_norm(ALPHA * res_ref[...] + mix, g_ref[...], b_ref[...])


def _proj_ln(lhs_list, w_list, res, g, b, tm):
    m, n = res.shape
    n_lhs = len(lhs_list)
    row = lambda i: (i, 0)
    const = lambda i: (0, 0)
    in_specs = ([pl.BlockSpec((tm, a.shape[1]), row) for a in lhs_list]
                + [pl.BlockSpec(w.shape, const) for w in w_list]
                + [pl.BlockSpec((tm, n), row), pl.BlockSpec((1, n), const), pl.BlockSpec((1, n), const)])
    return pl.pallas_call(
        functools.partial(_proj_ln_kernel, n_lhs=n_lhs),
        grid=(m // tm,),
        in_specs=in_specs,
        out_specs=pl.BlockSpec((tm, n), row),
        out_shape=jax.ShapeDtypeStruct((m, n), F32),
        compiler_params=_cparams(("parallel",), 40),
        name="out_proj_ln",
    )(*lhs_list, *w_list, res, g.reshape(1, n), b.reshape(1, n))


FFN_TF = 256
FFN_HALO = 16


def _ffn_tail(j, nf, act, wd_ref, x_ref, g_ref, b_ref, o_ref, acc_ref):
    acc_ref[...] += _dot(act.astype(BF16), wd_ref[...])

    @pl.when(j == nf - 1)
    def _():
        o_ref[...] = _layer_norm(ALPHA * x_ref[...] + acc_ref[...], g_ref[...], b_ref[...])


def _ffn_long_kernel(x_ref, xh_ref, wa_ref, wb_ref, cwa_ref, cwb_ref, cba_ref, cbb_ref, wd_ref, g_ref, b_ref,
                     o_ref, ha_ref, hb_ref, xs_ref, acc_ref, *, tm, tiles_per_seq, nf):
    i = pl.program_id(0)
    j = pl.program_id(1)

    @pl.when(j == 0)
    def _():
        xs_ref[0:FFN_HALO] = xh_ref[...].astype(BF16)
        xs_ref[FFN_HALO:] = x_ref[...].astype(BF16)
        acc_ref[...] = jnp.zeros_like(acc_ref)

    seq_start = (i % tiles_per_seq) == 0
    xs = xs_ref[...]

    def branch(w_ref, cw_ref, cb_ref, hout_ref):
        h = _dot(xs, w_ref[...])
        top = jnp.where(seq_start, 0.0, h[0:FFN_HALO])
        hx = jnp.concatenate([top, h[FFN_HALO:]], axis=0)
        hout_ref[0] = hx[tm + FFN_HALO - 8:]
        cw = cw_ref[...]
        return (cb_ref[...] + cw[2:3] * hx[FFN_HALO:] + cw[1:2] * pltpu.roll(hx, 1, 0)[FFN_HALO:]
                + cw[0:1] * pltpu.roll(hx, 2, 0)[FFN_HALO:])

    a = branch(wa_ref, cwa_ref, cba_ref, ha_ref)
    gate = branch(wb_ref, cwb_ref, cbb_ref, hb_ref)
    _ffn_tail(j, nf, (a * _sigmoid(a)) * gate, wd_ref, x_ref, g_ref, b_ref, o_ref, acc_ref)


def _ffn_short_kernel(x_ref, wa_ref, wb_ref, p1a_ref, p2a_ref, p1b_ref, p2b_ref, cwa_ref, cwb_ref, cba_ref,
                      cbb_ref, wd_ref, g_ref, b_ref, o_ref, ha_ref, hb_ref, xs_ref, acc_ref, *, seq_len, nf):
    j = pl.program_id(1)

    @pl.when(j == 0)
    def _():
        xs_ref[...] = x_ref[...].astype(BF16)
        acc_ref[...] = jnp.zeros_like(acc_ref)

    xs = xs_ref[...]

    def branch(w_ref, p1_ref, p2_ref, cw_ref, cb_ref, hout_ref):
        h = _dot(xs, w_ref[...])
        hout_ref[...] = h
        pos = lax.broadcasted_iota(jnp.int32, h.shape, 0) & (seq_len - 1)
        r1 = jnp.where(pos < 1, p1_ref[...], pltpu.roll(h, 1, 0))
        r2 = jnp.where(pos < 2, p2_ref[...], pltpu.roll(h, 2, 0))
        cw = cw_ref[...]
        return cb_ref[...] + cw[2:3] * h + cw[1:2] * r1 + cw[0:1] * r2

    a = branch(wa_ref, p1a_ref, p2a_ref, cwa_ref, cba_ref, ha_ref)
    gate = branch(wb_ref, p1b_ref, p2b_ref, cwb_ref, cbb_ref, hb_ref)
    _ffn_tail(j, nf, (a * _sigmoid(a)) * gate, wd_ref, x_ref, g_ref, b_ref, o_ref, acc_ref)


def _ffn_weight_specs(nf, tf):
    return dict(
        wa=pl.BlockSpec((D_MODEL, tf), lambda i, j: (0, j)),
        wb=pl.BlockSpec((D_MODEL, tf), lambda i, j: (0, nf + j)),
        cwa=pl.BlockSpec((FFN_CONV, tf), lambda i, j: (0, j)),
        cwb=pl.BlockSpec((FFN_CONV, tf), lambda i, j: (0, nf + j)),
        cba=pl.BlockSpec((1, tf), lambda i, j: (0, j)),
        cbb=pl.BlockSpec((1, tf), lambda i, j: (0, nf + j)),
        wd=pl.BlockSpec((tf, D_MODEL), lambda i, j: (j, 0)),
        vec=pl.BlockSpec((1, D_MODEL), lambda i, j: (0, 0)),
    )


def _ffn_long(x, seq_len, w_up, conv_w, conv_b, w_down, g, b, tm):
    m = x.shape[0]
    nseq = m // seq_len
    tf = FFN_TF
    nf = D_FF // tf
    tiles_per_seq = seq_len // tm
    ws = _ffn_weight_specs(nf, tf)
    hb_blocks = tm // FFN_HALO
    out, ha, hb = pl.pallas_call(
        functools.partial(_ffn_long_kernel, tm=tm, tiles_per_seq=tiles_per_seq, nf=nf),
        grid=(m // tm, nf),
        in_specs=[
            pl.BlockSpec((tm, D_MODEL), lambda i, j: (i, 0)),
            pl.BlockSpec((FFN_HALO, D_MODEL), lambda i, j: (jnp.maximum(i * hb_blocks - 1, 0), 0)),
            ws["wa"], ws["wb"], ws["cwa"], ws["cwb"], ws["cba"], ws["cbb"], ws["wd"], ws["vec"], ws["vec"],
        ],
        out_specs=[
            pl.BlockSpec((tm, D_MODEL), lambda i, j: (i, 0)),
            pl.BlockSpec((1, 8, tf), lambda i, j: (i, 0, j)),
            pl.BlockSpec((1, 8, tf), lambda i, j: (i, 0, j)),
        ],
        out_shape=[
            jax.ShapeDtypeStruct((m, D_MODEL), F32),
            jax.ShapeDtypeStruct((m // tm, 8, D_FF), F32),
            jax.ShapeDtypeStruct((m // tm, 8, D_FF), F32),
        ],
        scratch_shapes=[pltpu.VMEM((tm + FFN_HALO, D_MODEL), BF16), pltpu.VMEM((tm, D_MODEL), F32)],
        compiler_params=_cparams(("arbitrary", "arbitrary"), 52),
        name="conv_ffn_long",
    )(x, x, w_up, w_up, conv_w, conv_w, conv_b, conv_b, w_down, g.reshape(1, -1), b.reshape(1, -1))
    tails = jnp.concatenate([ha, hb], axis=-1).reshape(nseq, tiles_per_seq, 8, 2 * D_FF)
    return out, tails[:, tiles_per_seq - 1, 8 - (FFN_CONV - 1):]


def _ffn_short(x, seq_len, buf, w_up, conv_w, conv_b, w_down, g, b):
    m = x.shape[0]
    nseq = m // seq_len
    assert seq_len & (seq_len - 1) == 0 and seq_len >= FFN_CONV - 1
    tf = FFN_TF
    nf = D_FF // tf
    ws = _ffn_weight_specs(nf, tf)
    zeros = jnp.zeros((nseq, seq_len, 2 * D_FF), F32)
    p1 = zeros.at[:, 0].set(buf[:, 1]).reshape(m, 2 * D_FF)
    p2 = zeros.at[:, 0].set(buf[:, 0]).at[:, 1].set(buf[:, 1]).reshape(m, 2 * D_FF)
    pa = pl.BlockSpec((m, tf), lambda i, j: (0, j))
    pb = pl.BlockSpec((m, tf), lambda i, j: (0, nf + j))
    out, ha, hb = pl.pallas_call(
        functools.partial(_ffn_short_kernel, seq_len=seq_len, nf=nf),
        grid=(1, nf),
        in_specs=[
            pl.BlockSpec((m, D_MODEL), lambda i, j: (0, 0)),
            ws["wa"], ws["wb"], pa, pa, pb, pb, ws["cwa"], ws["cwb"], ws["cba"], ws["cbb"], ws["wd"],
            ws["vec"], ws["vec"],
        ],
        out_specs=[
            pl.BlockSpec((m, D_MODEL), lambda i, j: (0, 0)),
            pl.BlockSpec((m, tf), lambda i, j: (0, j)),
            pl.BlockSpec((m, tf), lambda i, j: (0, j)),
        ],
        out_shape=[
            jax.ShapeDtypeStruct((m, D_MODEL), F32),
            jax.ShapeDtypeStruct((m, D_FF), F32),
            jax.ShapeDtypeStruct((m, D_FF), F32),
        ],
        scratch_shapes=[pltpu.VMEM((m, D_MODEL), BF16), pltpu.VMEM((m, D_MODEL), F32)],
        compiler_params=_cparams(("arbitrary", "arbitrary"), 40),
        name="conv_ffn_short",
    )(x, w_up, w_up, p1, p2, p1, p2, conv_w, conv_w, conv_b, conv_b, w_down, g.reshape(1, -1), b.reshape(1, -1))
    h = jnp.concatenate([ha, hb], axis=-1).reshape(nseq, seq_len, 2 * D_FF)
    return out, h[:, seq_len - (FFN_CONV - 1):]


def _mla_proj_kernel(x_ref, win_ref, gq_ref, gkv_ref, wqn_ref, wqr_ref, wuk_ref, cos_ref, sa_ref, sb_ref,
                     q_ref, kc_ref, ckv_ref, kr_ref):
    cosv, sa, sb = cos_ref[...], sa_ref[...], sb_ref[...]
    half = QK_ROPE // 2

    def rope(v):
        return v * cosv + pltpu.roll(v, 128 - half, 1) * sa + pltpu.roll(v, half, 1) * sb

    h = _dot(x_ref[...].astype(BF16), win_ref[...])
    cq = _rms_norm(h[:, :Q_LORA], gq_ref[...])
    ckv = _rms_norm(h[:, Q_LORA:Q_LORA + KV_LORA], gkv_ref[...])
    kr = rope(h[:, Q_LORA + KV_LORA:])
    ckv_ref[...] = ckv
    kr_ref[...] = kr[:, :QK_ROPE]
    kc_ref[:, :KV_LORA] = ckv.astype(kc_ref.dtype)
    kc_ref[:, KV_LORA:] = kr.astype(kc_ref.dtype)
    cqb = cq.astype(BF16)
    qn = _dot(cqb, wqn_ref[...])
    qr = _dot(cqb, wqr_ref[...])
    for hd in range(MLA_HEADS):
        sl = slice(128 * hd, 128 * (hd + 1))
        q_ref[hd, :, :KV_LORA] = _dot(qn[:, sl].astype(BF16), wuk_ref[hd]).astype(q_ref.dtype)
        q_ref[hd, :, KV_LORA:] = rope(qr[:, sl]).astype(q_ref.dtype)


def _mla_proj(x, prm, tabs, tm, act_dtype):
    m = x.shape[0]
    cos, sa, sb = tabs
    ntab = cos.shape[0] // tm
    row = lambda i: (i, 0)
    c2 = lambda i: (0, 0)
    tab = pl.BlockSpec((tm, 128), lambda i: (i % ntab, 0))
    return pl.pallas_call(
        _mla_proj_kernel,
        grid=(m // tm,),
        in_specs=[
            pl.BlockSpec((tm, D_MODEL), row),
            pl.BlockSpec((D_MODEL, IN1_PAD), c2),
            pl.BlockSpec((1, Q_LORA), c2),
            pl.BlockSpec((1, KV_LORA), c2),
            pl.BlockSpec((Q_LORA, MLA_HEADS * 128), c2),
            pl.BlockSpec((Q_LORA, MLA_HEADS * 128), c2),
            pl.BlockSpec((MLA_HEADS, QK_NOPE, KV_LORA), lambda i: (0, 0, 0)),
            tab, tab, tab,
        ],
        out_specs=[
            pl.BlockSpec((MLA_HEADS, tm, QK_CAT), lambda i: (0, i, 0)),
            pl.BlockSpec((tm, QK_CAT), row),
            pl.BlockSpec((tm, KV_LORA), row),
            pl.BlockSpec((tm, QK_ROPE), row),
        ],
        out_shape=[
            jax.ShapeDtypeStruct((MLA_HEADS, m, QK_CAT), act_dtype),
            jax.ShapeDtypeStruct((m, QK_CAT), act_dtype),
            jax.ShapeDtypeStruct((m, KV_LORA), F32),
            jax.ShapeDtypeStruct((m, QK_ROPE), F32),
        ],
        compiler_params=_cparams(("parallel",), 48),
        name="mla_proj",
    )(x, prm["win"], prm["gq"], prm["gkv"], prm["wqn"], prm["wqr"], prm["wuk"], cos, sa, sb)


def _mla_prepare(w_in1, g_q, g_kv, w_uq, w_uk, w_uv):
    wq = w_uq.reshape(Q_LORA, MLA_HEADS, QK_NOPE + QK_ROPE)
    return dict(
        win=jnp.pad(w_in1, ((0, 0), (0, IN1_PAD - w_in1.shape[1]))).astype(BF16),
        gq=g_q.reshape(1, Q_LORA), gkv=g_kv.reshape(1, KV_LORA),
        wqn=wq[:, :, :QK_NOPE].reshape(Q_LORA, MLA_HEADS * QK_NOPE).astype(BF16),
        wqr=jnp.pad(wq[:, :, QK_NOPE:], ((0, 0), (0, 0), (0, 128 - QK_ROPE))).reshape(
            Q_LORA, MLA_HEADS * 128).astype(BF16),
        wuk=jnp.transpose(w_uk, (1, 2, 0)).astype(BF16),
        wuv=jnp.transpose(w_uv, (1, 0, 2)).astype(BF16),
    )


def _rope_tables(pos):
    half = QK_ROPE // 2
    inv = ROPE_BASE ** (-jnp.arange(0, QK_ROPE, 2, dtype=F32) / QK_ROPE)
    ang = pos.astype(F32)[:, None] * inv
    cos, sin = jnp.cos(ang), jnp.sin(ang)
    z = jnp.zeros_like(cos)
    cos_pad = jnp.concatenate([cos, cos, z, z], axis=1)
    sin_a = jnp.concatenate([-sin, z, z, z], axis=1)
    sin_b = jnp.concatenate([z, sin, z, z], axis=1)
    assert cos_pad.shape[1] == 128 and half * 4 == 128
    return cos_pad, sin_a, sin_b


def _softmax_step(s, v, m_ref, l_ref, acc_ref):
    m_prev = m_ref[...]
    m_new = jnp.maximum(m_prev, jnp.max(s, axis=-1, keepdims=True))
    a = jnp.exp(m_prev - m_new)
    p = jnp.exp(s - m_new)
    l_ref[...] = a * l_ref[...] + jnp.sum(p, axis=-1, keepdims=True)
    acc_ref[...] = a * acc_ref[...] + _dot(p.astype(BF16), v)
    m_ref[...] = m_new


def _softmax_init(m_ref, l_ref, acc_ref):
    m_ref[...] = jnp.full_like(m_ref, NEG_BIG)
    l_ref[...] = jnp.zeros_like(l_ref)
    acc_ref[...] = jnp.zeros_like(acc_ref)


def _attn_prompt_kernel(q_ref, k_ref, wuv_ref, o_ref, m_ref, l_ref, acc_ref, *, tq):
    qi = pl.program_id(1)
    rows = MLA_HEADS * tq
    q = q_ref[...].reshape(rows, QK_CAT)
    _softmax_init(m_ref, l_ref, acc_ref)

    def step(kb, masked):
        k = k_ref[pl.ds(pl.multiple_of(kb * tq, tq), tq), :]
        s = _dot_nt(q, k) * SM_SCALE
        if masked:
            tok = lax.broadcasted_iota(jnp.int32, s.shape, 0) & (tq - 1)
            col = lax.broadcasted_iota(jnp.int32, s.shape, 1)
            s = jnp.where(col <= tok, s, -jnp.inf)
        _softmax_step(s, k[:, :KV_LORA], m_ref, l_ref, acc_ref)

    def body(kb, carry):
        step(kb, False)
        return carry

    lax.fori_loop(0, qi, body, 0)
    step(qi, True)
    o = acc_ref[...] / l_ref[...]
    for hd in range(MLA_HEADS):
        o_ref[:, V_HEAD * hd:V_HEAD * (hd + 1)] = _dot(o[hd * tq:(hd + 1) * tq].astype(BF16), wuv_ref[hd])


def _attn_prompt(q, kcat, wuv, nseq, seq_len, tq):
    assert tq & (tq - 1) == 0
    nq = seq_len // tq
    rows = MLA_HEADS * tq
    return pl.pallas_call(
        functools.partial(_attn_prompt_kernel, tq=tq),
        grid=(nseq, nq),
        in_specs=[
            pl.BlockSpec((MLA_HEADS, tq, QK_CAT), lambda b, i: (0, b * nq + i, 0)),
            pl.BlockSpec((seq_len, QK_CAT), lambda b, i: (b, 0)),
            pl.BlockSpec((MLA_HEADS, KV_LORA, V_HEAD), lambda b, i: (0, 0, 0)),
        ],
        out_specs=pl.BlockSpec((tq, MLA_HEADS * V_HEAD), lambda b, i: (b * nq + i, 0)),
        out_shape=jax.ShapeDtypeStruct((nseq * seq_len, MLA_HEADS * V_HEAD), F32),
        scratch_shapes=[pltpu.VMEM((rows, 1), F32), pltpu.VMEM((rows, 1), F32), pltpu.VMEM((rows, KV_LORA), F32)],
        compiler_params=_cparams(("parallel", "arbitrary"), 48),
        name="mla_attn_prompt",
    )(q, kcat, wuv)


ATTN_PAGES_PER_STEP = 8


def _attn_sample_kernel(pt_ref, q_ref, kn_ref, *rest, npg, nsteps, nq):
    ckv_refs = rest[:npg]
    kr_refs = rest[npg:2 * npg]
    wuv_ref, o_ref, m_ref, l_ref, acc_ref = rest[2 * npg:]
    g = pl.program_id(1)
    rows = MLA_HEADS * nq
    q = q_ref[...].reshape(rows, QK_CAT)

    @pl.when(g == 0)
    def _():
        _softmax_init(m_ref, l_ref, acc_ref)

    kc = jnp.concatenate([r[...] for r in ckv_refs], axis=0).astype(BF16)
    kr = jnp.concatenate([r[...] for r in kr_refs], axis=0).astype(BF16)
    s = (_dot_nt(q[:, :KV_LORA].astype(BF16), kc)
         + _dot_nt(q[:, KV_LORA:KV_LORA + QK_ROPE].astype(BF16), kr)) * SM_SCALE
    _softmax_step(s, kc, m_ref, l_ref, acc_ref)

    @pl.when(g == nsteps - 1)
    def _():
        kn = jnp.concatenate([kn_ref[...], jnp.zeros((128 - nq, QK_CAT), F32)], axis=0).astype(BF16)
        sn = _dot_nt(q.astype(BF16), kn) * SM_SCALE
        tok = lax.broadcasted_iota(jnp.int32, sn.shape, 0) & (nq - 1)
        col = lax.broadcasted_iota(jnp.int32, sn.shape, 1)
        sn = jnp.where(col <= tok, sn, -jnp.inf)
        _softmax_step(sn, kn[:, :KV_LORA], m_ref, l_ref, acc_ref)
        o = acc_ref[...] / l_ref[...]
        for hd in range(MLA_HEADS):
            o_ref[:, V_HEAD * hd:V_HEAD * (hd + 1)] = _dot(o[hd * nq:(hd + 1) * nq].astype(BF16), wuv_ref[hd])


def _attn_sample(q, knew, cache_ckv, cache_krope, page_table, wuv, layer):
    nseq, npages = page_table.shape
    nq = knew.shape[0] // nseq
    assert nq & (nq - 1) == 0 and nq % 8 == 0 and nq <= 128
    npg = ATTN_PAGES_PER_STEP
    nsteps = npages // npg
    rows = MLA_HEADS * nq

    def page_spec(width, j):
        return pl.BlockSpec((None, None, PAGE_SIZE, width), lambda b, g, pt: (pt[b, g * npg + j], layer, 0, 0))

    grid_spec = pltpu.PrefetchScalarGridSpec(
        num_scalar_prefetch=1,
        grid=(nseq, nsteps),
        in_specs=([pl.BlockSpec((MLA_HEADS, nq, QK_CAT), lambda b, g, pt: (0, b, 0)),
                   pl.BlockSpec((nq, QK_CAT), lambda b, g, pt: (b, 0))]
                  + [page_spec(KV_LORA, j) for j in range(npg)]
                  + [page_spec(QK_ROPE, j) for j in range(npg)]
                  + [pl.BlockSpec((MLA_HEADS, KV_LORA, V_HEAD), lambda b, g, pt: (0, 0, 0))]),
        out_specs=pl.BlockSpec((nq, MLA_HEADS * V_HEAD), lambda b, g, pt: (b, 0)),
        scratch_shapes=[pltpu.VMEM((rows, 1), F32), pltpu.VMEM((rows, 1), F32), pltpu.VMEM((rows, KV_LORA), F32)],
    )
    return pl.pallas_call(
        functools.partial(_attn_sample_kernel, npg=npg, nsteps=nsteps, nq=nq),
        grid_spec=grid_spec,
        out_shape=jax.ShapeDtypeStruct((nseq * nq, MLA_HEADS * V_HEAD), F32),
        compiler_params=_cparams(("parallel", "arbitrary"), 40),
        name="mla_attn_sample",
    )(page_table, q, knew, *([cache_ckv] * npg), *([cache_krope] * npg), wuv)


def _run_group(x, pos0, st, caches, prm, *, long_rows):
    nseq, seq_len, _ = x.shape
    m = nseq * seq_len
    x2 = x.reshape(m, D_MODEL)
    tm = 512 if long_rows else m
    out = {}

    h0 = _linear(x2, prm["w_in0"], tm).reshape(nseq, seq_len, IN0_PAD)
    if st is None:
        s5r = jnp.zeros((nseq, 1, S5_NSTATE), F32)
        s5i = s5r
        ssd0 = jnp.zeros((nseq, SSD_INNER, SSD_STATE), F32)
        cbuf = jnp.zeros((nseq, 8, SSD_CONV_DIM), F32)
    else:
        s5r = st["s5_re"].reshape(nseq, 1, S5_NSTATE)
        s5i = st["s5_im"].reshape(nseq, 1, S5_NSTATE)
        ssd0 = st["ssd"].reshape(nseq, SSD_INNER, SSD_STATE)
        cbuf = jnp.pad(st["ssd_conv"], ((0, 0), (8 - (SSD_CONV - 1), 0), (0, 0)))
    y5, hr, hi = _s5_call(h0, s5r, s5i, prm["s5"], tc=min(256, seq_len))
    out["s5_re"] = hr.reshape(1, nseq, S5_GROUPS, S5_STATE)
    out["s5_im"] = hi.reshape(1, nseq, S5_GROUPS, S5_STATE)
    lp = -(-seq_len // SSD_CHUNK) * SSD_CHUNK
    h0p = h0 if lp == seq_len else jnp.pad(h0, ((0, 0), (0, lp - seq_len), (0, 0)))
    ys, hs = _ssd_call(h0p, min(seq_len, SSD_CHUNK), cbuf, ssd0, prm["ssd"])
    out["ssd"] = hs.reshape(1, nseq, SSD_HEADS, SSD_HEADDIM, SSD_STATE)
    xbc_raw = h0[:, :, 2 * S5_WIDTH:2 * S5_WIDTH + SSD_CONV_DIM]
    if st is None:
        out["ssd_conv"] = xbc_raw[:, seq_len - (SSD_CONV - 1):][None]
    else:
        out["ssd_conv"] = jnp.concatenate([st["ssd_conv"], xbc_raw], axis=1)[:, -(SSD_CONV - 1):][None]
    ys2 = ys[:, :seq_len].reshape(m, SSD_INNER)
    x2 = _proj_ln([y5.reshape(m, S5_WIDTH), ys2], prm["w_out0"], x2, prm["ln1_g"][0], prm["ln1_b"][0], tm)

    ffn_bufs = []
    for layer in range(DEPTH):
        if layer == 1:
            pos = pos0 + jnp.arange(seq_len, dtype=jnp.int32)
            tabs = _rope_tables(pos)
            if not long_rows:
                tabs = tuple(jnp.tile(t, (nseq, 1)) for t in tabs)
            act_dtype = BF16 if long_rows else F32
            q, kcat, ckv, kr = _mla_proj(x2, prm["mla"], tabs, tm, act_dtype)
            out["ckv"] = ckv.reshape(nseq, 1, seq_len, KV_LORA)
            out["krope"] = kr.reshape(nseq, 1, seq_len, QK_ROPE)
            if caches is None:
                o = _attn_prompt(q, kcat, prm["mla"]["wuv"], nseq, seq_len, tq=256)
            else:
                o = _attn_sample(q, kcat, caches[0], caches[1], caches[2], prm["mla"]["wuv"], layer=0)
            x2 = _proj_ln([o], [prm["w_out1"]], x2, prm["ln1_g"][1], prm["ln1_b"][1], tm)
        fw = prm["ffn"][layer]
        if long_rows:
            x2, fbuf = _ffn_long(x2, seq_len, fw["w_up"], fw["conv_w"], fw["conv_b"], fw["w_down"],
                                 prm["ln2_g"][layer], prm["ln2_b"][layer], tm=1024)
        else:
            x2, fbuf = _ffn_short(x2, seq_len, st["ffn_conv"][layer], fw["w_up"], fw["conv_w"], fw["conv_b"],
                                  fw["w_down"], prm["ln2_g"][layer], prm["ln2_b"][layer])
        ffn_bufs.append(fbuf)
    out["ffn_conv"] = jnp.stack(ffn_bufs)
    return x2.reshape(nseq, seq_len, D_MODEL), out


def kernel(x_prompt, x_sample, state_s5_re, state_s5_im, state_ssd, state_ssd_conv, state_ffn_conv, cache_ckv, cache_krope, page_table, w_in0, s5_a_re, s5_a_im, s5_log_dt, s5_b_re, s5_b_im, s5_c_re, s5_c_im, s5_d, s5_w_glu, s5_b_glu, ssd_conv_w, ssd_conv_b, ssd_dt_bias, ssd_a_log, ssd_d, ssd_norm_g, w_out0, w_in1, mla_q_norm_g, mla_kv_norm_g, mla_w_uq, mla_w_uk, mla_w_uv, w_out1, ln1_g, ln1_b, ffn_w_up, ffn_conv_w, ffn_conv_b, ffn_w_down, ln2_g, ln2_b):
    assert x_prompt.shape[-1] == D_MODEL and w_in0.shape == (1, D_MODEL, 2 * S5_WIDTH + SSD_CONV_DIM + SSD_HEADS)
    assert s5_b_re.shape == (1, S5_GROUPS, S5_STATE, S5_GROUP) and ffn_w_up.shape == (DEPTH, D_MODEL, 2 * D_FF)
    assert state_ssd.shape[2:] == (SSD_HEADS, SSD_HEADDIM, SSD_STATE) and cache_ckv.shape[2:] == (PAGE_SIZE, KV_LORA)
    assert mla_w_uk.shape == (1, KV_LORA, MLA_HEADS, QK_NOPE) and cache_krope.shape[2:] == (PAGE_SIZE, QK_ROPE)

    prm = dict(
        w_in0=jnp.pad(w_in0[0], ((0, 0), (0, IN0_PAD - w_in0.shape[2]))).astype(BF16),
        s5=_s5_prepare(s5_a_re[0], s5_a_im[0], s5_log_dt[0], s5_b_re[0], s5_b_im[0], s5_c_re[0], s5_c_im[0],
                       s5_d[0], s5_w_glu[0], s5_b_glu[0]),
        ssd=_ssd_prepare(ssd_conv_w[0], ssd_conv_b[0], ssd_dt_bias[0], ssd_a_log[0], ssd_d[0], ssd_norm_g[0]),
        w_out0=[w_out0[0, :S5_WIDTH].astype(BF16), w_out0[0, S5_WIDTH:].astype(BF16)],
        mla=_mla_prepare(w_in1[0], mla_q_norm_g[0], mla_kv_norm_g[0], mla_w_uq[0], mla_w_uk[0], mla_w_uv[0]),
        w_out1=w_out1[0].astype(BF16),
        ln1_g=ln1_g, ln1_b=ln1_b, ln2_g=ln2_g, ln2_b=ln2_b,
        ffn=[dict(w_up=ffn_w_up[i].astype(BF16), conv_w=ffn_conv_w[i], conv_b=ffn_conv_b[i].reshape(1, -1),
                  w_down=ffn_w_down[i].astype(BF16)) for i in range(DEPTH)],
    )

    y_p, op = _run_group(x_prompt, 0, None, None, prm, long_rows=True)
    past_len = page_table.shape[1] * PAGE_SIZE
    st = dict(s5_re=state_s5_re[0], s5_im=state_s5_im[0], ssd=state_ssd[0], ssd_conv=state_ssd_conv[0],
              ffn_conv=state_ffn_conv)
    y_s, os_ = _run_group(x_sample, past_len, st, (cache_ckv, cache_krope, page_table), prm, long_rows=False)
    return (y_p, y_s,
            op["s5_re"], op["s5_im"], op["ssd"], op["ssd_conv"], op["ckv"], op["krope"], op["ffn_conv"],
            os_["s5_re"], os_["s5_im"], os_["ssd"], os_["ssd_conv"], os_["ckv"], os_["krope"], os_["ffn_conv"])
```
